```python
import math
import jax
import jax.numpy as jnp
from jax import lax
import numpy as np

D_MODEL = 1024
BATCH = 1
SEQ = 16384
DEPTH = 1
DEC_BATCH = 128
DEC_SEQ = 4
PAST_LEN = 8192
PAGE_SIZE = 128

DN_HK = D_MODEL // 128
DN_HV = 2 * DN_HK
DN_DK = 128
DN_DV = 128
DN_CONV = 4
DN_CHUNK = 64
DN_QK = DN_HK * DN_DK
DN_V = DN_HV * DN_DV
DN_CONV_CH = 2 * DN_QK + DN_V
ATT_H = D_MODEL // 128
ATT_KV = 2
ATT_HD = 128
IDX_H = 8
IDX_D = 64
TOPK_MAX = 256
Q_BLOCK = 128
ROPE_THETA = 10000.0
D_FF = ((8 * D_MODEL // 3 + 127) // 128) * 128
FFN_CONV = 3
EPS = 1e-6

IN_SPLITS = (DN_CONV_CH, DN_V, DN_HV, DN_HV, ATT_H * ATT_HD, ATT_KV * ATT_HD, ATT_KV * ATT_HD,
             IDX_H * IDX_D, IDX_D, IDX_H, D_MODEL, D_MODEL)
IN_WIDTH = (DN_CONV_CH + DN_V + 2 * DN_HV + ATT_H * ATT_HD + 2 * ATT_KV * ATT_HD
            + IDX_H * IDX_D + IDX_D + IDX_H + 2 * D_MODEL)

kernel_name = 'hybrid_gdn_dsa_convffn_step'


def _rms_norm(x, w):
    xf = x.astype(jnp.float32)
    y = xf * lax.rsqrt(jnp.mean(xf * xf, axis=-1, keepdims=True) + EPS)
    return (y * w.astype(jnp.float32)).astype(x.dtype)


def _l2_norm(x):
    xf = x.astype(jnp.float32)
    return (xf * lax.rsqrt(jnp.sum(xf * xf, axis=-1, keepdims=True) + EPS)).astype(x.dtype)


def _split_cols(a, sizes):
    idx, off = [], 0
    for s in sizes[:-1]:
        off += s
        idx.append(off)
    return jnp.split(a, idx, axis=-1)


def _causal_dwconv(x, prev, w):
    W, T = w.shape[0], x.shape[1]
    xp = jnp.concatenate([prev.astype(x.dtype), x], axis=1)
    y = xp[:, 0:T] * w[0]
    for j in range(1, W):
        y = y + xp[:, j:j + T] * w[j]
    return y, xp[:, T:]


def _rope(x, pos):
    d = x.shape[-1]
    half = d // 2
    inv = jnp.power(jnp.float32(ROPE_THETA), -2.0 * jnp.arange(half, dtype=jnp.float32) / d)
    ang = pos.astype(jnp.float32)[:, None] * inv[None, :]
    cos = jnp.cos(ang)[:, None, :].astype(x.dtype)
    sin = jnp.sin(ang)[:, None, :].astype(x.dtype)
    x1, x2 = x[..., :half], x[..., half:]
    return jnp.concatenate([x1 * cos - x2 * sin, x2 * cos + x1 * sin], axis=-1)


def _to_chunks(a, c, n):
    a = a.astype(jnp.float32)
    a = jnp.pad(a, [(0, 0), (0, n * c - a.shape[1])] + [(0, 0)] * (a.ndim - 2))
    a = a.reshape(a.shape[0], n, c, *a.shape[2:])
    return a.transpose((1, 0, 3, 2) + tuple(range(4, a.ndim)))


def _gated_delta_rule(q, k, v, beta, g, s0):
    B, T, H, _ = q.shape
    DV = v.shape[-1]
    c = min(DN_CHUNK, T)
    n = -(-T // c)
    qc, kc, vc = _to_chunks(q, c, n), _to_chunks(k, c, n), _to_chunks(v, c, n)
    bc, gc = _to_chunks(beta, c, n), _to_chunks(g, c, n)
    gcum = jnp.cumsum(gc, axis=-1)
    incl = jnp.tril(jnp.ones((c, c), dtype=bool))
    strict = jnp.tril(jnp.ones((c, c), dtype=bool), -1)
    decay = jnp.exp(jnp.where(incl, gcum[..., :, None] - gcum[..., None, :], -jnp.inf))
    kb = kc * bc[..., None]
    a_low = jnp.where(strict, jnp.einsum('...id,...jd->...ij', kb, kc) * decay, 0.0)
    rhs = jnp.concatenate([vc * bc[..., None], kb * jnp.exp(gcum)[..., None]], axis=-1)
    sol = lax.linalg.triangular_solve(a_low, rhs, left_side=True, lower=True, unit_diagonal=True)
    u, w = sol[..., :DV], sol[..., DV:]
    intra = jnp.einsum('...id,...jd->...ij', qc, kc) * decay
    qd = qc * jnp.exp(gcum)[..., None]
    kd = kc * jnp.exp(gcum[..., -1:] - gcum)[..., None]
    glast = jnp.exp(gcum[..., -1])

    def step(S, xs):
        qd_c, kd_c, u_c, w_c, intra_c, gl = xs
        v_new = u_c - jnp.einsum('bhcd,bhde->bhce', w_c, S)
        o = jnp.einsum('bhcd,bhde->bhce', qd_c, S) + jnp.einsum('bhij,bhje->bhie', intra_c, v_new)
        S = S * gl[..., None, None] + jnp.einsum('bhcd,bhce->bhde', kd_c, v_new)
        return S, o

    S, o = lax.scan(step, s0.astype(jnp.float32), (qd, kd, u, w, intra, glast))
    o = o.transpose(1, 0, 3, 2, 4).reshape(B, n * c, H, DV)[:, :T]
    return o.astype(v.dtype), S.astype(s0.dtype)


def _indexer_scores(qi, wi, ki, q_pos, k_pos):
    s = jax.nn.relu(jnp.einsum('bthd,bsd->bths', qi, ki))
    scores = jnp.einsum('bths,bth->bts', s, wi).astype(jnp.float32)
    return jnp.where(k_pos[None, None, :] <= q_pos[None, :, None], scores, -jnp.inf)


def _sparse_softmax(q, k_sel, v_sel, valid):
    B, T, H, D = q.shape
    N = k_sel.shape[3]
    qg = q.reshape(B, T, N, H // N, D)
    s = jnp.einsum('btngd,btknd->btngk', qg, k_sel).astype(jnp.float32) * (D ** -0.5)
    s = jnp.where(valid[:, :, None, None, :], s, -jnp.inf)
    pr = jax.nn.softmax(s, axis=-1).astype(v_sel.dtype)
    return jnp.einsum('btngk,btknd->btngd', pr, v_sel).reshape(B, T, H * D)


def _dsa_prompt(q, k, v, qi, wi, ki, pos):
    B, S = q.shape[:2]
    qb = min(Q_BLOCK, S)
    nb = S // qb
    topk = min(TOPK_MAX, S // 4)

    def blocks(a):
        return a.reshape(B, nb, qb, *a.shape[2:]).swapaxes(0, 1)

    gather = jax.vmap(lambda a, i: a[i])

    def one(args):
        q_b, qi_b, wi_b, pos_b = args
        vals, idx = lax.top_k(_indexer_scores(qi_b, wi_b, ki, pos_b, pos), topk)
        return _sparse_softmax(q_b, gather(k, idx), gather(v, idx), jnp.isfinite(vals))

    o = lax.map(one, (blocks(q), blocks(qi), blocks(wi), pos.reshape(nb, qb)))
    return o.swapaxes(0, 1).reshape(B, S, -1)


def _dsa_sample(q, k, v, qi, wi, ki, pos, ck, cv, cki, page_table):
    DB, T = q.shape[:2]
    P = ck.shape[1]
    past = page_table.shape[1] * P
    ki_past = cki[page_table].reshape(DB, past, IDX_D).astype(ki.dtype)
    ki_all = jnp.concatenate([ki_past, ki], axis=1)
    L = past + T
    topk = min(TOPK_MAX, L // 4)
    vals, idx = lax.top_k(_indexer_scores(qi, wi, ki_all, pos, jnp.arange(L, dtype=jnp.int32)), topk)
    in_past = (idx < past)[..., None, None]
    safe = jnp.minimum(idx, past - 1)
    b = jnp.arange(DB)[:, None, None]
    phys = page_table[b, safe // P] * P + safe % P
    new_j = jnp.clip(idx - past, 0, T - 1)
    k_sel = jnp.where(in_past, ck.reshape(-1, ATT_KV, ATT_HD)[phys].astype(k.dtype), k[b, new_j])
    v_sel = jnp.where(in_past, cv.reshape(-1, ATT_KV, ATT_HD)[phys].astype(v.dtype), v[b, new_j])
    return _sparse_softmax(q, k_sel, v_sel, jnp.isfinite(vals))


def _layer(x, c, pos, dn_conv_prev, dn_state, ffn_prev, attend, p):
    B, T, _ = x.shape
    mod = (jax.nn.silu(c) @ p['w_ada'] + p['b_ada'])[:, None, :]
    sh1, sc1, g1, sh2, sc2, g2 = jnp.split(mod, 6, axis=-1)
    h = _rms_norm(x, p['norm1_w']) * (1 + sc1) + sh1
    (dn_qkv, dn_z, dn_b, dn_a, a_q, a_k, a_v, i_q, i_k, i_w, gl_dn, gl_att) = _split_cols(h @ p['w_in'], IN_SPLITS)
    dn_qkv, new_dn_conv = _causal_dwconv(dn_qkv, dn_conv_prev, p['dn_conv_w'])
    dq, dk, dv = _split_cols(jax.nn.silu(dn_qkv), (DN_QK, DN_QK, DN_V))
    rep = DN_HV // DN_HK
    dq = jnp.repeat(_l2_norm(dq.reshape(B, T, DN_HK, DN_DK)), rep, axis=2) * (DN_DK ** -0.5)
    dk = jnp.repeat(_l2_norm(dk.reshape(B, T, DN_HK, DN_DK)), rep, axis=2)
    dv = dv.reshape(B, T, DN_HV, DN_DV)
    beta = jax.nn.sigmoid(dn_b.astype(jnp.float32))
    g = -jnp.exp(p['dn_a_log'].astype(jnp.float32)) * jax.nn.softplus(
        dn_a.astype(jnp.float32) + p['dn_dt_bias'].astype(jnp.float32))
    o_dn, new_dn_state = _gated_delta_rule(dq, dk, dv, beta, g, dn_state)
    o_dn = (_rms_norm(o_dn, p['dn_norm_w']) * jax.nn.silu(dn_z.reshape(B, T, DN_HV, DN_DV))).reshape(B, T, DN_V)
    q = _rope(_rms_norm(a_q.reshape(B, T, ATT_H, ATT_HD), p['q_norm_w']), pos)
    k = _rope(_rms_norm(a_k.reshape(B, T, ATT_KV, ATT_HD), p['k_norm_w']), pos)
    v = a_v.reshape(B, T, ATT_KV, ATT_HD)
    iq = _rope(i_q.reshape(B, T, IDX_H, IDX_D), pos)
    ik = _rope(_rms_norm(i_k, p['idx_k_norm_w'])[:, :, None, :], pos)[:, :, 0, :]
    iw = i_w * (IDX_H ** -0.5 * IDX_D ** -0.5)
    o_att = attend(q, k, v, iq, iw, ik)
    merged = (jax.nn.sigmoid(gl_dn) * (o_dn @ p['w_dn_proj'])
              + jax.nn.sigmoid(gl_att) * (o_att @ p['w_att_proj']))
    x = x + g1 * (merged @ p['w_o'])
    h2 = _rms_norm(x, p['norm2_w']) * (1 + sc2) + sh2
    u, new_ffn = _causal_dwconv(h2 @ p['w_up'], ffn_prev, p['ffn_conv_w'])
    u_gate, u_val = jnp.split(u + p['ffn_conv_b'], 2, axis=-1)
    x = x + g2 * ((jax.nn.silu(u_gate) * u_val) @ p['w_down'])
    return x, (k, v, ik, new_dn_state, new_dn_conv, new_ffn)


def setup_inputs(seed: int = 0) -> dict:
    key = jax.random.key(seed)
    ks = jax.random.split(key, 40)
    f32 = jnp.float32
    n_pages = PAST_LEN // PAGE_SIZE
    n_used = DEC_BATCH * n_pages
    n_pool = n_used + max(1, n_used // 4)

    def nrm(k, shape, scale=1.0):
        return jax.random.normal(k, shape, f32) * scale

    dt = jnp.exp(jax.random.uniform(ks[17], (DEPTH, DN_HV), f32, math.log(1e-3), math.log(1e-1)))
    return {
        'x_prompt': nrm(ks[0], (BATCH, SEQ, D_MODEL)),
        'x_sample': nrm(ks[1], (DEC_BATCH, DEC_SEQ, D_MODEL)),
        'c_prompt': nrm(ks[2], (BATCH, D_MODEL)),
        'c_sample': nrm(ks[3], (DEC_BATCH, D_MODEL)),
        'cache_k': nrm(ks[4], (DEPTH, n_pool, PAGE_SIZE, ATT_KV, ATT_HD)),
        'cache_v': nrm(ks[5], (DEPTH, n_pool, PAGE_SIZE, ATT_KV, ATT_HD)),
        'cache_kidx': nrm(ks[6], (DEPTH, n_pool, PAGE_SIZE, IDX_D)),
        'state_dn': nrm(ks[7], (DEPTH, DEC_BATCH, DN_HV, DN_DK, DN_DV), 0.1),
        'state_dn_conv': nrm(ks[8], (DEPTH, DEC_BATCH, DN_CONV - 1, DN_CONV_CH)),
        'state_ffn_conv': nrm(ks[9], (DEPTH, DEC_BATCH, FFN_CONV - 1, 2 * D_FF)),
        'page_table': jax.random.permutation(ks[10], n_pool)[:n_used].reshape(DEC_BATCH, n_pages).astype(jnp.int32),
        'w_ada': nrm(ks[11], (DEPTH, D_MODEL, 6 * D_MODEL), 0.5 * D_MODEL ** -0.5),
        'b_ada': nrm(ks[12], (DEPTH, 6 * D_MODEL), 0.02),
        'norm1_w': 1.0 + nrm(ks[13], (DEPTH, D_MODEL), 0.02),
        'w_in': nrm(ks[14], (DEPTH, D_MODEL, IN_WIDTH), D_MODEL ** -0.5),
        'dn_conv_w': nrm(ks[15], (DEPTH, DN_CONV, DN_CONV_CH), DN_CONV ** -0.5),
        'dn_a_log': jnp.log(jax.random.uniform(ks[16], (DEPTH, DN_HV), f32, 1.0, 16.0)),
        'dn_dt_bias': dt + jnp.log(-jnp.expm1(-dt)),
        'dn_norm_w': 1.0 + nrm(ks[18], (DEPTH, DN_DV), 0.02),
        'q_norm_w': 1.0 + nrm(ks[19], (DEPTH, ATT_HD), 0.02),
        'k_norm_w': 1.0 + nrm(ks[20], (DEPTH, ATT_HD), 0.02),
        'idx_k_norm_w': 1.0 + nrm(ks[21], (DEPTH, IDX_D), 0.02),
        'w_dn_proj': nrm(ks[22], (DEPTH, DN_V, D_MODEL), DN_V ** -0.5),
        'w_att_proj': nrm(ks[23], (DEPTH, ATT_H * ATT_HD, D_MODEL), (ATT_H * ATT_HD) ** -0.5),
        'w_o': nrm(ks[24], (DEPTH, D_MODEL, D_MODEL), D_MODEL ** -0.5),
        'norm2_w': 1.0 + nrm(ks[25], (DEPTH, D_MODEL), 0.02),
        'w_up': nrm(ks[26], (DEPTH, D_MODEL, 2 * D_FF), D_MODEL ** -0.5),
        'ffn_conv_w': nrm(ks[27], (DEPTH, FFN_CONV, 2 * D_FF), FFN_CONV ** -0.5),
        'ffn_conv_b': nrm(ks[28], (DEPTH, 2 * D_FF), 0.02),
        'w_down': nrm(ks[29], (DEPTH, D_FF, D_MODEL), D_FF ** -0.5),
    }


def reference(x_prompt, x_sample, c_prompt, c_sample, cache_k, cache_v, cache_kidx, state_dn, state_dn_conv,
              state_ffn_conv, page_table, w_ada, b_ada, norm1_w, w_in, dn_conv_w, dn_a_log, dn_dt_bias, dn_norm_w,
              q_norm_w, k_norm_w, idx_k_norm_w, w_dn_proj, w_att_proj, w_o, norm2_w, w_up, ffn_conv_w, ffn_conv_b,
              w_down):
    B, S, _ = x_prompt.shape
    DB, T, _ = x_sample.shape
    past = page_table.shape[1] * cache_k.shape[2]
    pos_p = jnp.arange(S, dtype=jnp.int32)
    pos_s = past + jnp.arange(T, dtype=jnp.int32)
    hp, hs = x_prompt, x_sample
    outs_p, outs_s = [], []
    for l in range(DEPTH):
        p = {'w_ada': w_ada[l], 'b_ada': b_ada[l], 'norm1_w': norm1_w[l], 'w_in': w_in[l],
             'dn_conv_w': dn_conv_w[l], 'dn_a_log': dn_a_log[l], 'dn_dt_bias': dn_dt_bias[l],
             'dn_norm_w': dn_norm_w[l], 'q_norm_w': q_norm_w[l], 'k_norm_w': k_norm_w[l],
             'idx_k_norm_w': idx_k_norm_w[l], 'w_dn_proj': w_dn_proj[l], 'w_att_proj': w_att_proj[l],
             'w_o': w_o[l], 'norm2_w': norm2_w[l], 'w_up': w_up[l], 'ffn_conv_w': ffn_conv_w[l],
             'ffn_conv_b': ffn_conv_b[l], 'w_down': w_down[l]}
        hp, st_p = _layer(
            hp, c_prompt, pos_p,
            jnp.zeros((B, DN_CONV - 1, DN_CONV_CH), x_prompt.dtype),
            jnp.zeros((B, DN_HV, DN_DK, DN_DV), state_dn.dtype),
            jnp.zeros((B, FFN_CONV - 1, 2 * D_FF), x_prompt.dtype),
            lambda q, k, v, qi, wi, ki: _dsa_prompt(q, k, v, qi, wi, ki, pos_p),
            p)
        hs, st_s = _layer(
            hs, c_sample, pos_s, state_dn_conv[l], state_dn[l], state_ffn_conv[l],
            lambda q, k, v, qi, wi, ki: _dsa_sample(q, k, v, qi, wi, ki, pos_s, cache_k[l], cache_v[l],
                                                    cache_kidx[l], page_table),
            p)
        outs_p.append(st_p)
        outs_s.append(st_s)

    def stack(outs, i):
        return jnp.stack([o[i] for o in outs], axis=0)

    return (hp, hs,
            stack(outs_p, 0), stack(outs_p, 1), stack(outs_p, 2), stack(outs_p, 3), stack(outs_p, 4), stack(outs_p, 5),
            stack(outs_s, 0), stack(outs_s, 1), stack(outs_s, 2), stack(outs_s, 3), stack(outs_s, 4), stack(outs_s, 5))
```

```python
import functools
import math

import jax
import jax.numpy as jnp
from jax import lax
from jax.experimental import pallas as pl
from jax.experimental.pallas import tpu as pltpu

F32 = jnp.float32
BF16 = jnp.bfloat16

HEAD_DIM = 128
ATT_KV = 2
IDX_H = 8
IDX_D = 64
DN_CONV_W = 4
FFN_CONV_W = 3
DN_CHUNK = 64
TOPK_MAX = 256
Q_BLOCK = 128
ROPE_THETA = 10000.0
EPS = 1e-6
VMEM_LIMIT = 56 * 1024 * 1024


def _cparams(n_axes):
    return pltpu.CompilerParams(dimension_semantics=("arbitrary",) * n_axes,
                                vmem_limit_bytes=VMEM_LIMIT)


def _ada_kernel(c_ref, w_ref, b_ref, o_ref):
    c = c_ref[...]
    s = c * jax.nn.sigmoid(c)
    o_ref[...] = jnp.dot(s.astype(BF16), w_ref[...], preferred_element_type=F32) + b_ref[...]


def _ada(c, w_bf, b, tn=1536):
    rows, d = c.shape
    n = w_bf.shape[1]
    return pl.pallas_call(
        _ada_kernel,
        grid=(n // tn,),
        in_specs=[pl.BlockSpec((rows, d), lambda j: (0, 0)),
                  pl.BlockSpec((d, tn), lambda j: (0, j)),
                  pl.BlockSpec((1, tn), lambda j: (0, j))],
        out_specs=pl.BlockSpec((rows, tn), lambda j: (0, j)),
        out_shape=jax.ShapeDtypeStruct((rows, n), F32),
        compiler_params=_cparams(1),
        name="ada_mod",
    )(c, w_bf, b)


def _norm_mm_kernel(x_ref, sc_ref, sh_ref, nw_ref, w_ref, o_ref, h_ref):
    @pl.when(pl.program_id(1) == 0)
    def _():
        x = x_ref[...]
        y = x * lax.rsqrt(jnp.mean(x * x, axis=-1, keepdims=True) + EPS) * nw_ref[...]
        h_ref[...] = (y * (1.0 + sc_ref[...]) + sh_ref[...]).astype(BF16)

    o_ref[...] = jnp.dot(h_ref[...], w_ref[...], preferred_element_type=F32)


def _norm_mm(x, sc, sh, nw, w_bf, tm, tn, name):
    t, d = x.shape
    n = w_bf.shape[1]
    bm = sc.shape[0]
    return pl.pallas_call(
        _norm_mm_kernel,
        grid=(t // tm, n // tn),
        in_specs=[pl.BlockSpec((tm, d), lambda i, j: (i, 0)),
                  pl.BlockSpec((bm, d), lambda i, j: (0, 0)),
                  pl.BlockSpec((bm, d), lambda i, j: (0, 0)),
                  pl.BlockSpec((1, d), lambda i, j: (0, 0)),
                  pl.BlockSpec((d, tn), lambda i, j: (0, j))],
        out_specs=pl.BlockSpec((tm, tn), lambda i, j: (i, j)),
        out_shape=jax.ShapeDtypeStruct((t, n), F32),
        scratch_shapes=[pltpu.VMEM((tm, d), BF16)],
        compiler_params=_cparams(2),
        name=name,
    )(x, sc, sh, nw, w_bf)


def _merge_kernel(odn_ref, oatt_ref, gl_ref, x_ref, g1_ref, wdn_ref, watt_ref, wo_ref, o_ref):
    d = x_ref.shape[1]
    a = jnp.dot(odn_ref[...].astype(BF16), wdn_ref[...], preferred_element_type=F32)
    b = jnp.dot(oatt_ref[...].astype(BF16), watt_ref[...], preferred_element_type=F32)
    merged = jax.nn.sigmoid(gl_ref[:, :d]) * a + jax.nn.sigmoid(gl_ref[:, d:]) * b
    y = jnp.dot(merged.astype(BF16), wo_ref[...], preferred_element_type=F32)
    o_ref[...] = x_ref[...] + g1_ref[...] * y


def _merge(o_dn, o_att, gl, x, g1, wdn_bf, watt_bf, wo_bf, tm):
    t, d = x.shape
    bm = g1.shape[0]
    full = lambda a: pl.BlockSpec(a.shape, lambda i: (0, 0))
    return pl.pallas_call(
        _merge_kernel,
        grid=(t // tm,),
        in_specs=[pl.BlockSpec((tm, o_dn.shape[1]), lambda i: (i, 0)),
                  pl.BlockSpec((tm, o_att.shape[1]), lambda i: (i, 0)),
                  pl.BlockSpec((tm, 2 * d), lambda i: (i, 0)),
                  pl.BlockSpec((tm, d), lambda i: (i, 0)),
                  pl.BlockSpec((bm, d), lambda i: (0, 0)),
                  full(wdn_bf), full(watt_bf), full(wo_bf)],
        out_specs=pl.BlockSpec((tm, d), lambda i: (i, 0)),
        out_shape=jax.ShapeDtypeStruct((t, d), F32),
        compiler_params=_cparams(1),
        name="merge_out",
    )(o_dn, o_att, gl, x, g1, wdn_bf, watt_bf, wo_bf)


def _ffn_kernel(up_ref, prev_ref, cw_ref, cb_ref, wd_ref, x_ref, g2_ref, o_ref, ext_ref, act_ref,
                *, rs, tm, hp, cc):
    w = FFN_CONV_W
    halo = (w - 1) * rs
    dff = wd_ref.shape[0]

    @pl.when(pl.program_id(0) == 0)
    def _():
        ext_ref[hp - halo:hp, :] = prev_ref[...]

    ext_ref[hp:hp + tm, :] = up_ref[...]

    def conv(c0):
        acc = None
        for j in range(w):
            r0 = hp - (w - 1 - j) * rs
            term = ext_ref[r0:r0 + tm, c0:c0 + cc] * cw_ref[j:j + 1, c0:c0 + cc]
            acc = term if acc is None else acc + term
        return acc + cb_ref[:, c0:c0 + cc]

    for c in range(dff // cc):
        ug = conv(c * cc)
        uv = conv(dff + c * cc)
        act_ref[:, c * cc:(c + 1) * cc] = (ug * jax.nn.sigmoid(ug) * uv).astype(BF16)

    y = jnp.dot(act_ref[...], wd_ref[...], preferred_element_type=F32)
    o_ref[...] = x_ref[...] + g2_ref[...] * y
    ext_ref[hp - halo:hp, :] = ext_ref[hp + tm - halo:hp + tm, :]


def _ffn_tail(up, prev, cw, cb, wd_bf, x1, g2, tm, rs):
    t, c2 = up.shape
    d = x1.shape[1]
    dff = wd_bf.shape[0]
    bm = g2.shape[0]
    halo = (FFN_CONV_W - 1) * rs
    hp = -(-halo // 8) * 8
    kern = functools.partial(_ffn_kernel, rs=rs, tm=tm, hp=hp, cc=256)
    return pl.pallas_call(
        kern,
        grid=(t // tm,),
        in_specs=[pl.BlockSpec((tm, c2), lambda i: (i, 0)),
                  pl.BlockSpec((halo, c2), lambda i: (0, 0)),
                  pl.BlockSpec((FFN_CONV_W, c2), lambda i: (0, 0)),
                  pl.BlockSpec((1, c2), lambda i: (0, 0)),
                  pl.BlockSpec((dff, d), lambda i: (0, 0)),
                  pl.BlockSpec((tm, d), lambda i: (i, 0)),
                  pl.BlockSpec((bm, d), lambda i: (0, 0))],
        out_specs=pl.BlockSpec((tm, d), lambda i: (i, 0)),
        out_shape=jax.ShapeDtypeStruct((t, d), F32),
        scratch_shapes=[pltpu.VMEM((hp + tm, c2), F32), pltpu.VMEM((tm, dff), BF16)],
        compiler_params=_cparams(1),
        name="ffn_tail",
    )(up, prev, cw, cb, wd_bf, x1, g2)


def _rms_norm(x, w):
    xf = x.astype(F32)
    y = xf * lax.rsqrt(jnp.mean(xf * xf, axis=-1, keepdims=True) + EPS)
    return (y * w.astype(F32)).astype(x.dtype)


def _l2_norm(x):
    xf = x.astype(F32)
    return (xf * lax.rsqrt(jnp.sum(xf * xf, axis=-1, keepdims=True) + EPS)).astype(x.dtype)


def _causal_dwconv(x, prev, w):
    W, T = w.shape[0], x.shape[1]
    xp = jnp.concatenate([prev.astype(x.dtype), x], axis=1)
    y = xp[:, 0:T] * w[0]
    for j in range(1, W):
        y = y + xp[:, j:j + T] * w[j]
    return y, xp[:, T:]


def _rope(x, pos):
    d = x.shape[-1]
    half = d // 2
    inv = jnp.power(jnp.float32(ROPE_THETA), -2.0 * jnp.arange(half, dtype=jnp.float32) / d)
    ang = pos.astype(jnp.float32)[:, None] * inv[None, :]
    cos = jnp.cos(ang)[:, None, :].astype(x.dtype)
    sin = jnp.sin(ang)[:, None, :].astype(x.dtype)
    x1, x2 = x[..., :half], x[..., half:]
    return jnp.concatenate([x1 * cos - x2 * sin, x2 * cos + x1 * sin], axis=-1)


def _to_chunks(a, c, n):
    a = a.astype(jnp.float32)
    a = jnp.pad(a, [(0, 0), (0, n * c - a.shape[1])] + [(0, 0)] * (a.ndim - 2))
    a = a.reshape(a.shape[0], n, c, *a.shape[2:])
    return a.transpose((1, 0, 3, 2) + tuple(range(4, a.ndim)))


def _gated_delta_rule(q, k, v, beta, g, s0):
    B, T, H, _ = q.shape
    DV = v.shape[-1]
    c = min(DN_CHUNK, T)
    n = -(-T // c)
    qc, kc, vc = _to_chunks(q, c, n), _to_chunks(k, c, n), _to_chunks(v, c, n)
    bc, gc = _to_chunks(beta, c, n), _to_chunks(g, c, n)
    gcum = jnp.cumsum(gc, axis=-1)
    incl = jnp.tril(jnp.ones((c, c), dtype=bool))
    strict = jnp.tril(jnp.ones((c, c), dtype=bool), -1)
    decay = jnp.exp(jnp.where(incl, gcum[..., :, None] - gcum[..., None, :], -jnp.inf))
    kb = kc * bc[..., None]
    a_low = jnp.where(strict, jnp.einsum('...id,...jd->...ij', kb, kc) * decay, 0.0)
    rhs = jnp.concatenate([vc * bc[..., None], kb * jnp.exp(gcum)[..., None]], axis=-1)
    sol = lax.linalg.triangular_solve(a_low, rhs, left_side=True, lower=True, unit_diagonal=True)
    u, w = sol[..., :DV], sol[..., DV:]
    intra = jnp.einsum('...id,...jd->...ij', qc, kc) * decay
    qd = qc * jnp.exp(gcum)[..., None]
    kd = kc * jnp.exp(gcum[..., -1:] - gcum)[..., None]
    glast = jnp.exp(gcum[..., -1])

    def step(S, xs):
        qd_c, kd_c, u_c, w_c, intra_c, gl = xs
        v_new = u_c - jnp.einsum('bhcd,bhde->bhce', w_c, S)
        o = jnp.einsum('bhcd,bhde->bhce', qd_c, S) + jnp.einsum('bhij,bhje->bhie', intra_c, v_new)
        S = S * gl[..., None, None] + jnp.einsum('bhcd,bhce->bhde', kd_c, v_new)
        return S, o

    S, o = lax.scan(step, s0.astype(jnp.float32), (qd, kd, u, w, intra, glast))
    o = o.transpose(1, 0, 3, 2, 4).reshape(B, n * c, H, DV)[:, :T]
    return o.astype(v.dtype), S.astype(s0.dtype)


def _indexer_scores(qi, wi, ki, q_pos, k_pos):
    s = jax.nn.relu(jnp.einsum('bthd,bsd->bths', qi, ki))
    scores = jnp.einsum('bths,bth->bts', s, wi).astype(jnp.float32)
    return jnp.where(k_pos[None, None, :] <= q_pos[None, :, None], scores, -jnp.inf)


def _sparse_softmax(q, k_sel, v_sel, valid):
    B, T, H, D = q.shape
    N = k_sel.shape[3]
    qg = q.reshape(B, T, N, H // N, D)
    s = jnp.einsum('btngd,btknd->btngk', qg, k_sel).astype(jnp.float32) * (D ** -0.5)
    s = jnp.where(valid[:, :, None, None, :], s, -jnp.inf)
    pr = jax.nn.softmax(s, axis=-1).astype(v_sel.dtype)
    return jnp.einsum('btngk,btknd->btngd', pr, v_sel).reshape(B, T, H * D)


def _dsa_prompt(q, k, v, qi, wi, ki, pos):
    B, S = q.shape[:2]
    qb = min(Q_BLOCK, S)
    nb = S // qb
    topk = min(TOPK_MAX, S // 4)

    def blocks(a):
        return a.reshape(B, nb, qb, *a.shape[2:]).swapaxes(0, 1)

    gather = jax.vmap(lambda a, i: a[i])

    def one(args):
        q_b, qi_b, wi_b, pos_b = args
        vals, idx = lax.top_k(_indexer_scores(qi_b, wi_b, ki, pos_b, pos), topk)
        return _sparse_softmax(q_b, gather(k, idx), gather(v, idx), jnp.isfinite(vals))

    o = lax.map(one, (blocks(q), blocks(qi), blocks(wi), pos.reshape(nb, qb)))
    return o.swapaxes(0, 1).reshape(B, S, -1)


def _dsa_sample(q, k, v, qi, wi, ki, pos, ck, cv, cki, page_table):
    DB, T = q.shape[:2]
    P = ck.shape[1]
    past = page_table.shape[1] * P
    ki_past = cki[page_table].reshape(DB, past, IDX_D).astype(ki.dtype)
    ki_all = jnp.concatenate([ki_past, ki], axis=1)
    L = past + T
    topk = min(TOPK_MAX, L // 4)
    vals, idx = lax.top_k(_indexer_scores(qi, wi, ki_all, pos, jnp.arange(L, dtype=jnp.int32)), topk)
    in_past = (idx < past)[..., None, None]
    safe = jnp.minimum(idx, past - 1)
    b = jnp.arange(DB)[:, None, None]
    phys = page_table[b, safe // P] * P + safe % P
    new_j = jnp.clip(idx - past, 0, T - 1)
    k_sel = jnp.where(in_past, ck.reshape(-1, ATT_KV, HEAD_DIM)[phys].astype(k.dtype), k[b, new_j])
    v_sel = jnp.where(in_past, cv.reshape(-1, ATT_KV, HEAD_DIM)[phys].astype(v.dtype), v[b, new_j])
    return _sparse_softmax(q, k_sel, v_sel, jnp.isfinite(vals))


def _mixers_jax(B, T, d, dn_qkv, dn_z, dn_b, dn_a, a_q, a_k, a_v, i_q, i_k, i_w, pos,
                dn_conv_prev, dn_state, attend, p):
    dn_hk = d // HEAD_DIM
    dn_hv = 2 * dn_hk
    dn_qk = dn_hk * HEAD_DIM
    att_h = d // HEAD_DIM
    dn_qkv, new_dn_conv = _causal_dwconv(dn_qkv, dn_conv_prev, p['dn_conv_w'])
    sq = jax.nn.silu(dn_qkv)
    dq, dk, dv = sq[..., :dn_qk], sq[..., dn_qk:2 * dn_qk], sq[..., 2 * dn_qk:]
    dq = jnp.repeat(_l2_norm(dq.reshape(B, T, dn_hk, HEAD_DIM)), 2, axis=2) * (HEAD_DIM ** -0.5)
    dk = jnp.repeat(_l2_norm(dk.reshape(B, T, dn_hk, HEAD_DIM)), 2, axis=2)
    dv = dv.reshape(B, T, dn_hv, HEAD_DIM)
    beta = jax.nn.sigmoid(dn_b)
    g = -jnp.exp(p['dn_a_log']) * jax.nn.softplus(dn_a + p['dn_dt_bias'])
    o_dn, new_dn_state = _gated_delta_rule(dq, dk, dv, beta, g, dn_state)
    o_dn = (_rms_norm(o_dn, p['dn_norm_w']) * jax.nn.silu(dn_z.reshape(B, T, dn_hv, HEAD_DIM))).reshape(B, T, -1)
    q = _rope(_rms_norm(a_q.reshape(B, T, att_h, HEAD_DIM), p['q_norm_w']), pos)
    k = _rope(_rms_norm(a_k.reshape(B, T, ATT_KV, HEAD_DIM), p['k_norm_w']), pos)
    v = a_v.reshape(B, T, ATT_KV, HEAD_DIM)
    iq = _rope(i_q.reshape(B, T, IDX_H, IDX_D), pos)
    ik = _rope(_rms_norm(i_k, p['idx_k_norm_w'])[:, :, None, :], pos)[:, :, 0, :]
    iw = i_w * (IDX_H ** -0.5 * IDX_D ** -0.5)
    o_att = attend(q, k, v, iq, iw, ik)
    return o_dn, o_att, (k, v, ik, new_dn_state, new_dn_conv)


def _pack_w_in(w_in, d):
    hk = d // HEAD_DIM
    hv = 2 * hk
    conv_ch = 2 * hk * HEAD_DIM + hv * HEAD_DIM
    dn_v = hv * HEAD_DIM
    sizes = (conv_ch, dn_v, hv, hv, hk * HEAD_DIM, ATT_KV * HEAD_DIM, ATT_KV * HEAD_DIM,
             IDX_H * IDX_D, IDX_D, IDX_H, d, d)
    offs = [0]
    for s in sizes:
        offs.append(offs[-1] + s)
    col = lambda i: w_in[:, offs[i]:offs[i + 1]]
    pad = lambda n: jnp.zeros((w_in.shape[0], n), w_in.dtype)
    ga = jnp.concatenate([col(0), col(1)], axis=1)
    gb = jnp.concatenate([col(4), col(5), col(6)], axis=1)
    gc = jnp.concatenate([col(7), col(8), pad(64)], axis=1)
    gs = jnp.concatenate([col(2), col(3), col(9), pad(128 - 2 * hv - IDX_H)], axis=1)
    gd = jnp.concatenate([col(10), col(11)], axis=1)
    return tuple(g.astype(BF16) for g in (ga, gb, gc, gs, gd))


def _layer(x2, mod, B, T, time_major, pos, dn_conv_prev, dn_state, ffn_prev, attend, p, wts, tm):
    rows, d = x2.shape
    hk = d // HEAD_DIM
    hv = 2 * hk
    sh1, sc1, g1, sh2, sc2, g2 = [mod[:, i * d:(i + 1) * d] for i in range(6)]
    ga_w, gb_w, gc_w, gs_w, gd_w, wdn, watt, wo, wup, wdown = wts
    nw1 = p['norm1_w'][None, :]
    ga = _norm_mm(x2, sc1, sh1, nw1, ga_w, tm, 1536, "in_proj_a")
    gb = _norm_mm(x2, sc1, sh1, nw1, gb_w, tm, gb_w.shape[1], "in_proj_b")
    gc = _norm_mm(x2, sc1, sh1, nw1, gc_w, tm, gc_w.shape[1], "in_proj_c")
    gs = _norm_mm(x2, sc1, sh1, nw1, gs_w, tm, gs_w.shape[1], "in_proj_s")
    gd = _norm_mm(x2, sc1, sh1, nw1, gd_w, tm, 1024, "in_proj_d")

    def bt(a):
        if time_major:
            return a.reshape(T, B, -1).swapaxes(0, 1)
        return a.reshape(B, T, -1)

    def rows_of(a):
        if time_major:
            return a.swapaxes(0, 1).reshape(T * B, -1)
        return a.reshape(B * T, -1)

    conv_ch = 4 * hk * HEAD_DIM
    o_dn, o_att, st = _mixers_jax(
        B, T, d, bt(ga[:, :conv_ch]), bt(ga[:, conv_ch:]), bt(gs[:, :hv]), bt(gs[:, hv:2 * hv]),
        bt(gb[:, :hk * HEAD_DIM]), bt(gb[:, hk * HEAD_DIM:(hk + ATT_KV) * HEAD_DIM]),
        bt(gb[:, (hk + ATT_KV) * HEAD_DIM:]), bt(gc[:, :IDX_H * IDX_D]),
        bt(gc[:, IDX_H * IDX_D:IDX_H * IDX_D + IDX_D]), bt(gs[:, 2 * hv:2 * hv + IDX_H]),
        pos, dn_conv_prev, dn_state, attend, p)
    x1 = _merge(rows_of(o_dn), rows_of(o_att), gd, x2, g1, wdn, watt, wo, tm)

    up = _norm_mm(x1, sc2, sh2, p['norm2_w'][None, :], wup, tm, 1408, "ffn_up")
    rs = B if time_major else 1
    if time_major:
        prev_rows = ffn_prev.swapaxes(0, 1).reshape((FFN_CONV_W - 1) * B, -1)
        new_ffn = up[rows - (FFN_CONV_W - 1) * B:].reshape(FFN_CONV_W - 1, B, -1).swapaxes(0, 1)
        tm_ffn = B
    else:
        prev_rows = ffn_prev.reshape(FFN_CONV_W - 1, -1)
        new_ffn = up[rows - (FFN_CONV_W - 1):][None]
        tm_ffn = 256
    y = _ffn_tail(up, prev_rows, p['ffn_conv_w'], p['ffn_conv_b'][None, :], wdown, x1, g2, tm_ffn, rs)
    return y, st + (new_ffn,)


def kernel(x_prompt, x_sample, c_prompt, c_sample, cache_k, cache_v, cache_kidx, state_dn, state_dn_conv, state_ffn_conv, page_table, w_ada, b_ada, norm1_w, w_in, dn_conv_w, dn_a_log, dn_dt_bias, dn_norm_w, q_norm_w, k_norm_w, idx_k_norm_w, w_dn_proj, w_att_proj, w_o, norm2_w, w_up, ffn_conv_w, ffn_conv_b, w_down):
    B, S, d = x_prompt.shape
    DB, T, _ = x_sample.shape
    depth = w_ada.shape[0]
    hk = d // HEAD_DIM
    hv = 2 * hk
    past = page_table.shape[1] * cache_k.shape[2]
    pos_p = jnp.arange(S, dtype=jnp.int32)
    pos_s = past + jnp.arange(T, dtype=jnp.int32)
    hp = x_prompt.reshape(B * S, d)
    hs = x_sample.swapaxes(0, 1).reshape(T * DB, d)
    outs_p, outs_s = [], []
    for l in range(depth):
        p = {'norm1_w': norm1_w[l], 'dn_conv_w': dn_conv_w[l], 'dn_a_log': dn_a_log[l],
             'dn_dt_bias': dn_dt_bias[l], 'dn_norm_w': dn_norm_w[l], 'q_norm_w': q_norm_w[l],
             'k_norm_w': k_norm_w[l], 'idx_k_norm_w': idx_k_norm_w[l], 'norm2_w': norm2_w[l],
             'ffn_conv_w': ffn_conv_w[l], 'ffn_conv_b': ffn_conv_b[l]}
        wts = _pack_w_in(w_in[l], d) + tuple(
            w.astype(BF16) for w in (w_dn_proj[l], w_att_proj[l], w_o[l], w_up[l], w_down[l]))
        n_c = DB + B
        n_pad = -(-n_c // 8) * 8
        c_all = jnp.concatenate([c_sample, c_prompt, jnp.zeros((n_pad - n_c, d), F32)], axis=0)
        mod = _ada(c_all, w_ada[l].astype(BF16), b_ada[l][None, :])
        mod_s, mod_p = mod[:DB], mod[DB:DB + B]
        hp, st_p = _layer(
            hp, mod_p, B, S, False, pos_p,
            jnp.zeros((B, DN_CONV_W - 1, 4 * hk * HEAD_DIM), F32),
            jnp.zeros((B, hv, HEAD_DIM, HEAD_DIM), F32),
            jnp.zeros((B, FFN_CONV_W - 1, w_up.shape[2]), F32),
            lambda q, k, v, qi, wi, ki: _dsa_prompt(q, k, v, qi, wi, ki, pos_p),
            p, wts, 512)
        hs, st_s = _layer(
            hs, mod_s, DB, T, True, pos_s, state_dn_conv[l], state_dn[l], state_ffn_conv[l],
            lambda q, k, v, qi, wi, ki: _dsa_sample(q, k, v, qi, wi, ki, pos_s, cache_k[l], cache_v[l],
                                                    cache_kidx[l], page_table),
            p, wts, DB)
        outs_p.append(st_p)
        outs_s.append(st_s)

    def stack(outs, i):
        return jnp.stack([o[i] for o in outs], axis=0)

    y_p = hp.reshape(B, S, d)
    y_s = hs.reshape(T, DB, d).swapaxes(0, 1)
    return (y_p, y_s,
            stack(outs_p, 0), stack(outs_p, 1), stack(outs_p, 2), stack(outs_p, 3), stack(outs_p, 4), stack(outs_p, 5),
            stack(outs_s, 0), stack(outs_s, 1), stack(outs_s, 2), stack(outs_s, 3), stack(outs_s, 4), stack(outs_s, 5))
```

```python
import functools
import math

import jax
import jax.numpy as jnp
from jax import lax
from jax.experimental import pallas as pl
from jax.experimental.pallas import tpu as pltpu

F32 = jnp.float32
BF16 = jnp.bfloat16

HEAD_DIM = 128
ATT_KV = 2
IDX_H = 8
IDX_D = 64
DN_CONV_W = 4
FFN_CONV_W = 3
DN_CHUNK = 64
TOPK_MAX = 256
Q_BLOCK = 128
ROPE_THETA = 10000.0
EPS = 1e-6
VMEM_LIMIT = 56 * 1024 * 1024


def _cparams(n_axes):
    return pltpu.CompilerParams(dimension_semantics=("arbitrary",) * n_axes,
                                vmem_limit_bytes=VMEM_LIMIT)


def _ada_kernel(c_ref, w_ref, b_ref, o_ref):
    c = c_ref[...]
    s = c * jax.nn.sigmoid(c)
    o_ref[...] = jnp.dot(s.astype(BF16), w_ref[...], preferred_element_type=F32) + b_ref[...]


def _ada(c, w_bf, b, tn=1536):
    rows, d = c.shape
    n = w_bf.shape[1]
    return pl.pallas_call(
        _ada_kernel,
        grid=(n // tn,),
        in_specs=[pl.BlockSpec((rows, d), lambda j: (0, 0)),
                  pl.BlockSpec((d, tn), lambda j: (0, j)),
                  pl.BlockSpec((1, tn), lambda j: (0, j))],
        out_specs=pl.BlockSpec((rows, tn), lambda j: (0, j)),
        out_shape=jax.ShapeDtypeStruct((rows, n), F32),
        compiler_params=_cparams(1),
        name="ada_mod",
    )(c, w_bf, b)


def _norm_mm_kernel(x_ref, sc_ref, sh_ref, nw_ref, w_ref, o_ref, h_ref):
    @pl.when(pl.program_id(1) == 0)
    def _():
        x = x_ref[...]
        y = x * lax.rsqrt(jnp.mean(x * x, axis=-1, keepdims=True) + EPS) * nw_ref[...]
        h_ref[...] = (y * (1.0 + sc_ref[...]) + sh_ref[...]).astype(BF16)

    o_ref[...] = jnp.dot(h_ref[...], w_ref[...], preferred_element_type=F32)


def _norm_mm(x, sc, sh, nw, w_bf, tm, tn, name):
    t, d = x.shape
    n = w_bf.shape[1]
    bm = sc.shape[0]
    return pl.pallas_call(
        _norm_mm_kernel,
        grid=(t // tm, n // tn),
        in_specs=[pl.BlockSpec((tm, d), lambda i, j: (i, 0)),
                  pl.BlockSpec((bm, d), lambda i, j: (0, 0)),
                  pl.BlockSpec((bm, d), lambda i, j: (0, 0)),
                  pl.BlockSpec((1, d), lambda i, j: (0, 0)),
                  pl.BlockSpec((d, tn), lambda i, j: (0, j))],
        out_specs=pl.BlockSpec((tm, tn), lambda i, j: (i, j)),
        out_shape=jax.ShapeDtypeStruct((t, n), F32),
        scratch_shapes=[pltpu.VMEM((tm, d), BF16)],
        compiler_params=_cparams(2),
        name=name,
    )(x, sc, sh, nw, w_bf)


def _merge_kernel(odn_ref, oatt_ref, gl_ref, x_ref, g1_ref, wdn_ref, watt_ref, wo_ref, o_ref):
    d = x_ref.shape[1]
    a = jnp.dot(odn_ref[...].astype(BF16), wdn_ref[...], preferred_element_type=F32)
    b = jnp.dot(oatt_ref[...].astype(BF16), watt_ref[...], preferred_element_type=F32)
    merged = jax.nn.sigmoid(gl_ref[:, :d]) * a + jax.nn.sigmoid(gl_ref[:, d:]) * b
    y = jnp.dot(merged.astype(BF16), wo_ref[...], preferred_element_type=F32)
    o_ref[...] = x_ref[...] + g1_ref[...] * y


def _merge(o_dn, o_att, gl, x, g1, wdn_bf, watt_bf, wo_bf, tm):
    t, d = x.shape
    bm = g1.shape[0]
    full = lambda a: pl.BlockSpec(a.shape, lambda i: (0, 0))
    return pl.pallas_call(
        _merge_kernel,
        grid=(t // tm,),
        in_specs=[pl.BlockSpec((tm, o_dn.shape[1]), lambda i: (i, 0)),
                  pl.BlockSpec((tm, o_att.shape[1]), lambda i: (i, 0)),
                  pl.BlockSpec((tm, 2 * d), lambda i: (i, 0)),
                  pl.BlockSpec((tm, d), lambda i: (i, 0)),
                  pl.BlockSpec((bm, d), lambda i: (0, 0)),
                  full(wdn_bf), full(watt_bf), full(wo_bf)],
        out_specs=pl.BlockSpec((tm, d), lambda i: (i, 0)),
        out_shape=jax.ShapeDtypeStruct((t, d), F32),
        compiler_params=_cparams(1),
        name="merge_out",
    )(o_dn, o_att, gl, x, g1, wdn_bf, watt_bf, wo_bf)


def _ffn_kernel(up_ref, prev_ref, cw_ref, cb_ref, wd_ref, x_ref, g2_ref, o_ref, ext_ref, act_ref,
                *, rs, tm, hp, cc):
    w = FFN_CONV_W
    halo = (w - 1) * rs
    dff = wd_ref.shape[0]

    @pl.when(pl.program_id(0) == 0)
    def _():
        ext_ref[hp - halo:hp, :] = prev_ref[...]

    ext_ref[hp:hp + tm, :] = up_ref[...]

    def conv(c0):
        acc = None
        for j in range(w):
            r0 = hp - (w - 1 - j) * rs
            term = ext_ref[r0:r0 + tm, c0:c0 + cc] * cw_ref[j:j + 1, c0:c0 + cc]
            acc = term if acc is None else acc + term
        return acc + cb_ref[:, c0:c0 + cc]

    for c in range(dff // cc):
        ug = conv(c * cc)
        uv = conv(dff + c * cc)
        act_ref[:, c * cc:(c + 1) * cc] = (ug * jax.nn.sigmoid(ug) * uv).astype(BF16)

    y = jnp.dot(act_ref[...], wd_ref[...], preferred_element_type=F32)
    o_ref[...] = x_ref[...] + g2_ref[...] * y
    ext_ref[hp - halo:hp, :] = ext_ref[hp + tm - halo:hp + tm, :]


def _ffn_tail(up, prev, cw, cb, wd_bf, x1, g2, tm, rs):
    t, c2 = up.shape
    d = x1.shape[1]
    dff = wd_bf.shape[0]
    bm = g2.shape[0]
    halo = (FFN_CONV_W - 1) * rs
    hp = -(-halo // 8) * 8
    kern = functools.partial(_ffn_kernel, rs=rs, tm=tm, hp=hp, cc=256)
    return pl.pallas_call(
        kern,
        grid=(t // tm,),
        in_specs=[pl.BlockSpec((tm, c2), lambda i: (i, 0)),
                  pl.BlockSpec((halo, c2), lambda i: (0, 0)),
                  pl.BlockSpec((FFN_CONV_W, c2), lambda i: (0, 0)),
                  pl.BlockSpec((1, c2), lambda i: (0, 0)),
                  pl.BlockSpec((dff, d), lambda i: (0, 0)),
                  pl.BlockSpec((tm, d), lambda i: (i, 0)),
                  pl.BlockSpec((bm, d), lambda i: (0, 0))],
        out_specs=pl.BlockSpec((tm, d), lambda i: (i, 0)),
        out_shape=jax.ShapeDtypeStruct((t, d), F32),
        scratch_shapes=[pltpu.VMEM((hp + tm, c2), F32), pltpu.VMEM((tm, dff), BF16)],
        compiler_params=_cparams(1),
        name="ffn_tail",
    )(up, prev, cw, cb, wd_bf, x1, g2)


INT_MIN = -2 ** 31
NEG_BIG = -1e30
NT_DIMS = (((1,), (1,)), ((), ()))


def _score_keys(score, valid):
    bits = pltpu.bitcast(score, jnp.int32)
    key = jnp.where(bits < 0, bits ^ jnp.int32(0x7FFFFFFF), bits)
    key = jnp.where(key == -1, 0, key)
    return jnp.where(valid, key, jnp.int32(INT_MIN))


def _kth_threshold(count_ge, rows, topk):
    def step(it, ans_u):
        cand_u = ans_u | lax.shift_left(jnp.int32(1), 31 - it)
        cnt = count_ge(cand_u ^ jnp.int32(INT_MIN))
        return jnp.where(cnt >= topk, cand_u, ans_u)

    ans_u = lax.fori_loop(0, 32, step, jnp.zeros((rows, 1), jnp.int32))
    return ans_u ^ jnp.int32(INT_MIN)


def _dsa_prompt_kernel(iq_ref, iw_ref, q_ref, ik_ref, k_ref, v_ref, o_ref,
                       sc_ref, m_ref, l_ref, acc_ref, *, qb, kb, topk):
    i = pl.program_id(0)
    n_ih = iq_ref.shape[0]
    n_h = q_ref.shape[0]
    n_kv = k_ref.shape[0]
    grp = n_h // n_kv
    nlc = kb // 128
    nv = ((i + 1) * qb + kb - 1) // kb
    qpos = i * qb + lax.broadcasted_iota(jnp.int32, (qb, kb), 0)
    kiota = lax.broadcasted_iota(jnp.int32, (qb, kb), 1)
    wcols = [iw_ref[:, h:h + 1] for h in range(n_ih)]

    def scores(j, carry):
        ikb = ik_ref[j]
        acc = jnp.zeros((qb, kb), F32)
        for h in range(n_ih):
            s = lax.dot_general(iq_ref[h], ikb, NT_DIMS, preferred_element_type=F32)
            acc = acc + jnp.maximum(s, 0.0) * wcols[h]
        sc_ref[j] = _score_keys(acc, j * kb + kiota <= qpos)
        return carry

    lax.fori_loop(0, nv, scores, 0)

    def lane_count(pred_fn):
        def body(j, c):
            blk = sc_ref[j]
            p = pred_fn(blk, j)
            for t in range(nlc):
                c = c + jnp.where(p[:, t * 128:(t + 1) * 128], 1.0, 0.0)
            return c
        c = lax.fori_loop(0, nv, body, jnp.zeros((qb, 128), F32))
        return jnp.sum(c, axis=1, keepdims=True)

    kth = _kth_threshold(lambda cand: lane_count(lambda blk, j: blk >= cand), qb, float(topk))
    thr = jnp.maximum(kth, jnp.int32(INT_MIN + 1))
    n_ge = lane_count(lambda blk, j: blk >= thr)
    over = n_ge > float(topk)

    @pl.when(jnp.max(jnp.where(over, 1.0, 0.0)) > 0.0)
    def _():
        need = float(topk) - lane_count(lambda blk, j: blk > thr)
        n_bits = max(1, (sc_ref.shape[0] * kb - 1).bit_length())

        def step(it, lo_hi):
            lo, hi = lo_hi
            mid = lax.shift_right_logical(lo + hi, 1)
            cnt = lane_count(lambda blk, j: (blk == thr) & (j * kb + kiota <= mid))
            ok = cnt >= need
            return jnp.where(ok, lo, mid + 1), jnp.where(ok, mid, hi)

        lo0 = jnp.zeros((qb, 1), jnp.int32)
        hi0 = jnp.full((qb, 1), sc_ref.shape[0] * kb - 1, jnp.int32)
        cut, _ = lax.fori_loop(0, n_bits, step, (lo0, hi0))
        cut = jnp.where(over, cut, jnp.int32(2 ** 30))

        def demote(j, carry):
            blk = sc_ref[j]
            sc_ref[j] = jnp.where((blk == thr) & (j * kb + kiota > cut), thr - 1, blk)
            return carry

        lax.fori_loop(0, nv, demote, 0)

    m_ref[...] = jnp.full(m_ref.shape, NEG_BIG, F32)
    l_ref[...] = jnp.zeros(l_ref.shape, F32)
    acc_ref[...] = jnp.zeros(acc_ref.shape, F32)

    def attend(j, carry):
        sel = sc_ref[j] >= thr
        for n in range(n_kv):
            kblk = k_ref[n, j]
            vblk = v_ref[n, j]
            for hh in range(grp):
                h = n * grp + hh
                s = lax.dot_general(q_ref[h], kblk, NT_DIMS, preferred_element_type=F32)
                sm = jnp.where(sel, s, NEG_BIG)
                m_old = m_ref[h]
                m_new = jnp.maximum(m_old, jnp.max(sm, axis=1, keepdims=True))
                p = jnp.exp(sm - m_new)
                alpha = jnp.exp(m_old - m_new)
                l_ref[h] = alpha * l_ref[h] + jnp.sum(p, axis=1, keepdims=True)
                acc_ref[h] = alpha * acc_ref[h] + jnp.dot(p.astype(BF16), vblk, preferred_element_type=F32)
                m_ref[h] = m_new
        return carry

    lax.fori_loop(0, nv, attend, 0)
    for h in range(n_h):
        o_ref[:, h * HEAD_DIM:(h + 1) * HEAD_DIM] = acc_ref[h] / l_ref[h]


def _dsa_prompt_pallas(q, k, v, iq, iw, ik, topk, qb=128, kb=256):
    s_len, n_h, hd = q.shape
    n_kv = k.shape[1]
    n_ih = iq.shape[1]
    kb = min(kb, s_len)
    qb = min(qb, s_len)
    nkb = s_len // kb
    q_b = (q * (hd ** -0.5)).astype(BF16).swapaxes(0, 1)
    k_b = k.astype(BF16).swapaxes(0, 1).reshape(n_kv, nkb, kb, hd)
    v_b = v.astype(BF16).swapaxes(0, 1).reshape(n_kv, nkb, kb, hd)
    iq_b = iq.astype(BF16).swapaxes(0, 1)
    ik_b = ik.astype(BF16).reshape(nkb, kb, ik.shape[-1])
    const = lambda a: pl.BlockSpec(a.shape, lambda i: (0,) * a.ndim, pipeline_mode=pl.Buffered(1))
    kern = functools.partial(_dsa_prompt_kernel, qb=qb, kb=kb, topk=topk)
    return pl.pallas_call(
        kern,
        grid=(s_len // qb,),
        in_specs=[pl.BlockSpec((n_ih, qb, iq.shape[-1]), lambda i: (0, i, 0)),
                  pl.BlockSpec((qb, n_ih), lambda i: (i, 0)),
                  pl.BlockSpec((n_h, qb, hd), lambda i: (0, i, 0)),
                  const(ik_b), const(k_b), const(v_b)],
        out_specs=pl.BlockSpec((qb, n_h * hd), lambda i: (i, 0)),
        out_shape=jax.ShapeDtypeStruct((s_len, n_h * hd), F32),
        scratch_shapes=[pltpu.VMEM((nkb, qb, kb), jnp.int32),
                        pltpu.VMEM((n_h, qb, 1), F32),
                        pltpu.VMEM((n_h, qb, 1), F32),
                        pltpu.VMEM((n_h, qb, hd), F32)],
        compiler_params=_cparams(1),
        name="dsa_prompt",
    )(iq_b, iw, q_b, ik_b, k_b, v_b)


def _page_copies(b, j, pt_ref, src_ref, buf_ref, sem):
    return pltpu.make_async_copy(src_ref.at[pt_ref[b, j]], buf_ref.at[j], sem)


def _dsa_sample_kernel(pt_ref, iq_ref, iw_ref, q_ref, ikn_ref, kn_ref, vn_ref, cki_ref, ck_ref, cv_ref,
                       o_ref, kibuf, kbuf, vbuf, sc_ref, sems, *, n_pages, page, t_new, tp, topk):
    b = pl.program_id(0)
    nb = pl.num_programs(0)
    slot = b % 2
    n_kv = q_ref.shape[1]
    grp = q_ref.shape[2] // tp
    n_ih = iq_ref.shape[1] // tp
    past = n_pages * page
    ncol = past // 128 + 1

    def start_all(bb, src_ref, buf_ref, sem):
        def body(j, c):
            _page_copies(bb, j, pt_ref, src_ref, buf_ref, sem).start()
            return c
        lax.fori_loop(0, n_pages, body, 0)

    def wait_all(bb, src_ref, buf_ref, sem):
        def body(j, c):
            _page_copies(bb, j, pt_ref, src_ref, buf_ref, sem).wait()
            return c
        lax.fori_loop(0, n_pages, body, 0)

    @pl.when(b == 0)
    def _():
        start_all(b, ck_ref, kbuf.at[slot], sems.at[slot])

    start_all(b, cki_ref, kibuf, sems.at[2])
    start_all(b, cv_ref, vbuf, sems.at[3])

    @pl.when(b + 1 < nb)
    def _():
        start_all(b + 1, ck_ref, kbuf.at[1 - slot], sems.at[1 - slot])

    wait_all(b, cki_ref, kibuf, sems.at[2])
    iq = iq_ref[0]
    iw = iw_ref[0]

    def head_sum(s):
        s = jnp.maximum(s, 0.0) * iw
        acc = s[0:tp]
        for h in range(1, n_ih):
            acc = acc + s[h * tp:(h + 1) * tp]
        return acc

    ki = kibuf[...].reshape(past, kibuf.shape[-1]).astype(BF16)
    sp = head_sum(lax.dot_general(iq, ki, NT_DIMS, preferred_element_type=F32))
    sn = head_sum(lax.dot_general(iq, ikn_ref[0], NT_DIMS, preferred_element_type=F32))
    row = lax.broadcasted_iota(jnp.int32, (tp, 128), 0)
    lane = lax.broadcasted_iota(jnp.int32, (tp, 128), 1)
    sc_ref[:, :past] = _score_keys(sp, jnp.full(sp.shape, True))
    sc_ref[:, past:] = _score_keys(sn, (lane <= row) & (lane < t_new))

    def lane_count(pred_fn):
        c = jnp.zeros((tp, 128), F32)
        for j in range(ncol):
            c = c + jnp.where(pred_fn(sc_ref[:, j * 128:(j + 1) * 128], j), 1.0, 0.0)
        return jnp.sum(c, axis=1, keepdims=True)

    kth = _kth_threshold(lambda cand: lane_count(lambda blk, j: blk >= cand), tp, float(topk))
    thr = jnp.maximum(kth, jnp.int32(INT_MIN + 1))
    real_row = lax.broadcasted_iota(jnp.int32, (tp, 1), 0) < t_new
    over = (lane_count(lambda blk, j: blk >= thr) > float(topk)) & real_row

    @pl.when(jnp.max(jnp.where(over, 1.0, 0.0)) > 0.0)
    def _():
        need = float(topk) - lane_count(lambda blk, j: blk > thr)
        n_bits = max(1, (ncol * 128 - 1).bit_length())

        def step(it, lo_hi):
            lo, hi = lo_hi
            mid = lax.shift_right_logical(lo + hi, 1)
            cnt = lane_count(lambda blk, j: (blk == thr) & (j * 128 + lane <= mid))
            ok = cnt >= need
            return jnp.where(ok, lo, mid + 1), jnp.where(ok, mid, hi)

        cut, _ = lax.fori_loop(0, n_bits, step, (jnp.zeros((tp, 1), jnp.int32),
                                                 jnp.full((tp, 1), ncol * 128 - 1, jnp.int32)))
        cut = jnp.where(over, cut, jnp.int32(2 ** 30))
        for j in range(ncol):
            blk = sc_ref[:, j * 128:(j + 1) * 128]
            sc_ref[:, j * 128:(j + 1) * 128] = jnp.where((blk == thr) & (j * 128 + lane > cut), thr - 1, blk)

    sel = sc_ref[...] >= thr
    sel_g = jnp.concatenate([jnp.where(sel, 1.0, 0.0)] * grp, axis=0) > 0.5

    wait_all(b, ck_ref, kbuf.at[slot], sems.at[slot])
    kall = kbuf[slot].reshape(past, kbuf.shape[-1])
    s_list = []
    for n in range(n_kv):
        kn = kall[:, n * HEAD_DIM:(n + 1) * HEAD_DIM].astype(BF16)
        qn = q_ref[0, n]
        s_past = lax.dot_general(qn, kn, NT_DIMS, preferred_element_type=F32)
        s_new = lax.dot_general(qn, kn_ref[0, n], NT_DIMS, preferred_element_type=F32)
        sm_p = jnp.where(sel_g[:, :past], s_past, NEG_BIG)
        sm_n = jnp.where(sel_g[:, past:], s_new, NEG_BIG)
        m = jnp.maximum(jnp.max(sm_p, axis=1, keepdims=True), jnp.max(sm_n, axis=1, keepdims=True))
        p_p = jnp.exp(sm_p - m)
        p_n = jnp.exp(sm_n - m)
        l = jnp.sum(p_p, axis=1, keepdims=True) + jnp.sum(p_n, axis=1, keepdims=True)
        s_list.append((p_p.astype(BF16), p_n.astype(BF16), l))

    wait_all(b, cv_ref, vbuf, sems.at[3])
    vall = vbuf[...].reshape(past, vbuf.shape[-1])
    for n in range(n_kv):
        p_p, p_n, l = s_list[n]
        vn = vall[:, n * HEAD_DIM:(n + 1) * HEAD_DIM].astype(BF16)
        o = (jnp.dot(p_p, vn, preferred_element_type=F32)
             + jnp.dot(p_n, vn_ref[0, n], preferred_element_type=F32))
        o_ref[0, n] = o / l


def _dsa_sample_pallas(q, k, v, iq, iw, ik, ck, cv, cki, page_table, topk):
    db, t_new, n_h, hd = q.shape
    n_kv = k.shape[2]
    grp = n_h // n_kv
    n_ih, idd = iq.shape[2], iq.shape[3]
    n_pages = page_table.shape[1]
    page = ck.shape[1]
    past = n_pages * page
    tp = 8
    padt = lambda a, n: jnp.pad(a, [(0, 0), (0, n - a.shape[1])] + [(0, 0)] * (a.ndim - 2))
    iq_s = padt(iq, tp).transpose(0, 2, 1, 3).reshape(db, n_ih * tp, idd).astype(BF16)
    iw_s = padt(iw, tp).transpose(0, 2, 1).reshape(db, n_ih * tp, 1)
    q_s = padt(q * (hd ** -0.5), tp).reshape(db, tp, n_kv, grp, hd).transpose(0, 2, 3, 1, 4)
    q_s = q_s.reshape(db, n_kv, grp * tp, hd).astype(BF16)
    ik_n = padt(ik, 128).astype(BF16)
    k_n = padt(k, 128).transpose(0, 2, 1, 3).astype(BF16)
    v_n = padt(v, 128).transpose(0, 2, 1, 3).astype(BF16)
    ck2 = ck.reshape(ck.shape[0], page, n_kv * hd)
    cv2 = cv.reshape(cv.shape[0], page, n_kv * hd)
    kern = functools.partial(_dsa_sample_kernel, n_pages=n_pages, page=page, t_new=t_new, tp=tp, topk=topk)
    blk = lambda a: pl.BlockSpec((1,) + a.shape[1:], lambda b, pt: (b,) + (0,) * (a.ndim - 1))
    hbm = pl.BlockSpec(memory_space=pl.ANY)
    out = pl.pallas_call(
        kern,
        grid_spec=pltpu.PrefetchScalarGridSpec(
            num_scalar_prefetch=1,
            grid=(db,),
            in_specs=[blk(iq_s), blk(iw_s), blk(q_s), blk(ik_n), blk(k_n), blk(v_n), hbm, hbm, hbm],
            out_specs=pl.BlockSpec((1, n_kv, grp * tp, hd), lambda b, pt: (b, 0, 0, 0)),
            scratch_shapes=[pltpu.VMEM((n_pages, page, idd), F32),
                            pltpu.VMEM((2, n_pages, page, n_kv * hd), F32),
                            pltpu.VMEM((n_pages, page, n_kv * hd), F32),
                            pltpu.VMEM((tp, past + 128), jnp.int32),
                            pltpu.SemaphoreType.DMA((4,))]),
        out_shape=jax.ShapeDtypeStruct((db, n_kv, grp * tp, hd), F32),
        compiler_params=_cparams(1),
        name="dsa_sample",
    )(page_table, iq_s, iw_s, q_s, ik_n, k_n, v_n, cki, ck2, cv2)
    out = out.reshape(db, n_kv, grp, tp, hd).transpose(0, 3, 1, 2, 4)[:, :t_new]
    return out.reshape(db, t_new, n_h * hd)


def _rms_norm(x, w):
    xf = x.astype(F32)
    y = xf * lax.rsqrt(jnp.mean(xf * xf, axis=-1, keepdims=True) + EPS)
    return (y * w.astype(F32)).astype(x.dtype)


def _l2_norm(x):
    xf = x.astype(F32)
    return (xf * lax.rsqrt(jnp.sum(xf * xf, axis=-1, keepdims=True) + EPS)).astype(x.dtype)


def _causal_dwconv(x, prev, w):
    W, T = w.shape[0], x.shape[1]
    xp = jnp.concatenate([prev.astype(x.dtype), x], axis=1)
    y = xp[:, 0:T] * w[0]
    for j in range(1, W):
        y = y + xp[:, j:j + T] * w[j]
    return y, xp[:, T:]


def _rope(x, pos):
    d = x.shape[-1]
    half = d // 2
    inv = jnp.power(jnp.float32(ROPE_THETA), -2.0 * jnp.arange(half, dtype=jnp.float32) / d)
    ang = pos.astype(jnp.float32)[:, None] * inv[None, :]
    cos = jnp.cos(ang)[:, None, :].astype(x.dtype)
    sin = jnp.sin(ang)[:, None, :].astype(x.dtype)
    x1, x2 = x[..., :half], x[..., half:]
    return jnp.concatenate([x1 * cos - x2 * sin, x2 * cos + x1 * sin], axis=-1)


def _to_chunks(a, c, n):
    a = a.astype(jnp.float32)
    a = jnp.pad(a, [(0, 0), (0, n * c - a.shape[1])] + [(0, 0)] * (a.ndim - 2))
    a = a.reshape(a.shape[0], n, c, *a.shape[2:])
    return a.transpose((1, 0, 3, 2) + tuple(range(4, a.ndim)))


def _gated_delta_rule(q, k, v, beta, g, s0):
    B, T, H, _ = q.shape
    DV = v.shape[-1]
    c = min(DN_CHUNK, T)
    n = -(-T // c)
    qc, kc, vc = _to_chunks(q, c, n), _to_chunks(k, c, n), _to_chunks(v, c, n)
    bc, gc = _to_chunks(beta, c, n), _to_chunks(g, c, n)
    gcum = jnp.cumsum(gc, axis=-1)
    incl = jnp.tril(jnp.ones((c, c), dtype=bool))
    strict = jnp.tril(jnp.ones((c, c), dtype=bool), -1)
    decay = jnp.exp(jnp.where(incl, gcum[..., :, None] - gcum[..., None, :], -jnp.inf))
    kb = kc * bc[..., None]
    a_low = jnp.where(strict, jnp.einsum('...id,...jd->...ij', kb, kc) * decay, 0.0)
    rhs = jnp.concatenate([vc * bc[..., None], kb * jnp.exp(gcum)[..., None]], axis=-1)
    sol = lax.linalg.triangular_solve(a_low, rhs, left_side=True, lower=True, unit_diagonal=True)
    u, w = sol[..., :DV], sol[..., DV:]
    intra = jnp.einsum('...id,...jd->...ij', qc, kc) * decay
    qd = qc * jnp.exp(gcum)[..., None]
    kd = kc * jnp.exp(gcum[..., -1:] - gcum)[..., None]
    glast = jnp.exp(gcum[..., -1])

    def step(S, xs):
        qd_c, kd_c, u_c, w_c, intra_c, gl = xs
        v_new = u_c - jnp.einsum('bhcd,bhde->bhce', w_c, S)
        o = jnp.einsum('bhcd,bhde->bhce', qd_c, S) + jnp.einsum('bhij,bhje->bhie', intra_c, v_new)
        S = S * gl[..., None, None] + jnp.einsum('bhcd,bhce->bhde', kd_c, v_new)
        return S, o

    S, o = lax.scan(step, s0.astype(jnp.float32), (qd, kd, u, w, intra, glast))
    o = o.transpose(1, 0, 3, 2, 4).reshape(B, n * c, H, DV)[:, :T]
    return o.astype(v.dtype), S.astype(s0.dtype)


def _indexer_scores(qi, wi, ki, q_pos, k_pos):
    s = jax.nn.relu(jnp.einsum('bthd,bsd->bths', qi, ki))
    scores = jnp.einsum('bths,bth->bts', s, wi).astype(jnp.float32)
    return jnp.where(k_pos[None, None, :] <= q_pos[None, :, None], scores, -jnp.inf)


def _sparse_softmax(q, k_sel, v_sel, valid):
    B, T, H, D = q.shape
    N = k_sel.shape[3]
    qg = q.reshape(B, T, N, H // N, D)
    s = jnp.einsum('btngd,btknd->btngk', qg, k_sel).astype(jnp.float32) * (D ** -0.5)
    s = jnp.where(valid[:, :, None, None, :], s, -jnp.inf)
    pr = jax.nn.softmax(s, axis=-1).astype(v_sel.dtype)
    return jnp.einsum('btngk,btknd->btngd', pr, v_sel).reshape(B, T, H * D)


def _dsa_prompt(q, k, v, qi, wi, ki, pos):
    B, S = q.shape[:2]
    qb = min(Q_BLOCK, S)
    nb = S // qb
    topk = min(TOPK_MAX, S // 4)

    def blocks(a):
        return a.reshape(B, nb, qb, *a.shape[2:]).swapaxes(0, 1)

    gather = jax.vmap(lambda a, i: a[i])

    def one(args):
        q_b, qi_b, wi_b, pos_b = args
        vals, idx = lax.top_k(_indexer_scores(qi_b, wi_b, ki, pos_b, pos), topk)
        return _sparse_softmax(q_b, gather(k, idx), gather(v, idx), jnp.isfinite(vals))

    o = lax.map(one, (blocks(q), blocks(qi), blocks(wi), pos.reshape(nb, qb)))
    return o.swapaxes(0, 1).reshape(B, S, -1)


def _dsa_sample(q, k, v, qi, wi, ki, pos, ck, cv, cki, page_table):
    DB, T = q.shape[:2]
    P = ck.shape[1]
    past = page_table.shape[1] * P
    ki_past = cki[page_table].reshape(DB, past, IDX_D).astype(ki.dtype)
    ki_all = jnp.concatenate([ki_past, ki], axis=1)
    L = past + T
    topk = min(TOPK_MAX, L // 4)
    vals, idx = lax.top_k(_indexer_scores(qi, wi, ki_all, pos, jnp.arange(L, dtype=jnp.int32)), topk)
    in_past = (idx < past)[..., None, None]
    safe = jnp.minimum(idx, past - 1)
    b = jnp.arange(DB)[:, None, None]
    phys = page_table[b, safe // P] * P + safe % P
    new_j = jnp.clip(idx - past, 0, T - 1)
    k_sel = jnp.where(in_past, ck.reshape(-1, ATT_KV, HEAD_DIM)[phys].astype(k.dtype), k[b, new_j])
    v_sel = jnp.where(in_past, cv.reshape(-1, ATT_KV, HEAD_DIM)[phys].astype(v.dtype), v[b, new_j])
    return _sparse_softmax(q, k_sel, v_sel, jnp.isfinite(vals))


def _mixers_jax(B, T, d, dn_qkv, dn_z, dn_b, dn_a, a_q, a_k, a_v, i_q, i_k, i_w, pos,
                dn_conv_prev, dn_state, attend, p):
    dn_hk = d // HEAD_DIM
    dn_hv = 2 * dn_hk
    dn_qk = dn_hk * HEAD_DIM
    att_h = d // HEAD_DIM
    dn_qkv, new_dn_conv = _causal_dwconv(dn_qkv, dn_conv_prev, p['dn_conv_w'])
    sq = jax.nn.silu(dn_qkv)
    dq, dk, dv = sq[..., :dn_qk], sq[..., dn_qk:2 * dn_qk], sq[..., 2 * dn_qk:]
    dq = jnp.repeat(_l2_norm(dq.reshape(B, T, dn_hk, HEAD_DIM)), 2, axis=2) * (HEAD_DIM ** -0.5)
    dk = jnp.repeat(_l2_norm(dk.reshape(B, T, dn_hk, HEAD_DIM)), 2, axis=2)
    dv = dv.reshape(B, T, dn_hv, HEAD_DIM)
    beta = jax.nn.sigmoid(dn_b)
    g = -jnp.exp(p['dn_a_log']) * jax.nn.softplus(dn_a + p['dn_dt_bias'])
    o_dn, new_dn_state = _gated_delta_rule(dq, dk, dv, beta, g, dn_state)
    o_dn = (_rms_norm(o_dn, p['dn_norm_w']) * jax.nn.silu(dn_z.reshape(B, T, dn_hv, HEAD_DIM))).reshape(B, T, -1)
    q = _rope(_rms_norm(a_q.reshape(B, T, att_h, HEAD_DIM), p['q_norm_w']), pos)
    k = _rope(_rms_norm(a_k.reshape(B, T, ATT_KV, HEAD_DIM), p['k_norm_w']), pos)
    v = a_v.reshape(B, T, ATT_KV, HEAD_DIM)
    iq = _rope(i_q.reshape(B, T, IDX_H, IDX_D), pos)
    ik = _rope(_rms_norm(i_k, p['idx_k_norm_w'])[:, :, None, :], pos)[:, :, 0, :]
    iw = i_w * (IDX_H ** -0.5 * IDX_D ** -0.5)
    o_att = attend(q, k, v, iq, iw, ik)
    return o_dn, o_att, (k, v, ik, new_dn_state, new_dn_conv)


def _pack_w_in(w_in, d):
    hk = d // HEAD_DIM
    hv = 2 * hk
    conv_ch = 2 * hk * HEAD_DIM + hv * HEAD_DIM
    dn_v = hv * HEAD_DIM
    sizes = (conv_ch, dn_v, hv, hv, hk * HEAD_DIM, ATT_KV * HEAD_DIM, ATT_KV * HEAD_DIM,
             IDX_H * IDX_D, IDX_D, IDX_H, d, d)
    offs = [0]
    for s in sizes:
        offs.append(offs[-1] + s)
    col = lambda i: w_in[:, offs[i]:offs[i + 1]]
    pad = lambda n: jnp.zeros((w_in.shape[0], n), w_in.dtype)
    ga = jnp.concatenate([col(0), col(1)], axis=1)
    gb = jnp.concatenate([col(4), col(5), col(6)], axis=1)
    gc = jnp.concatenate([col(7), col(8), pad(64)], axis=1)
    gs = jnp.concatenate([col(2), col(3), col(9), pad(128 - 2 * hv - IDX_H)], axis=1)
    gd = jnp.concatenate([col(10), col(11)], axis=1)
    return tuple(g.astype(BF16) for g in (ga, gb, gc, gs, gd))


def _layer(x2, mod, B, T, time_major, pos, dn_conv_prev, dn_state, ffn_prev, attend, p, wts, tm):
    rows, d = x2.shape
    hk = d // HEAD_DIM
    hv = 2 * hk
    sh1, sc1, g1, sh2, sc2, g2 = [mod[:, i * d:(i + 1) * d] for i in range(6)]
    ga_w, gb_w, gc_w, gs_w, gd_w, wdn, watt, wo, wup, wdown = wts
    nw1 = p['norm1_w'][None, :]
    ga = _norm_mm(x2, sc1, sh1, nw1, ga_w, tm, 1536, "in_proj_a")
    gb = _norm_mm(x2, sc1, sh1, nw1, gb_w, tm, gb_w.shape[1], "in_proj_b")
    gc = _norm_mm(x2, sc1, sh1, nw1, gc_w, tm, gc_w.shape[1], "in_proj_c")
    gs = _norm_mm(x2, sc1, sh1, nw1, gs_w, tm, gs_w.shape[1], "in_proj_s")
    gd = _norm_mm(x2, sc1, sh1, nw1, gd_w, tm, 1024, "in_proj_d")

    def bt(a):
        if time_major:
            return a.reshape(T, B, -1).swapaxes(0, 1)
        return a.reshape(B, T, -1)

    def rows_of(a):
        if time_major:
            return a.swapaxes(0, 1).reshape(T * B, -1)
        return a.reshape(B * T, -1)

    conv_ch = 4 * hk * HEAD_DIM
    o_dn, o_att, st = _mixers_jax(
        B, T, d, bt(ga[:, :conv_ch]), bt(ga[:, conv_ch:]), bt(gs[:, :hv]), bt(gs[:, hv:2 * hv]),
        bt(gb[:, :hk * HEAD_DIM]), bt(gb[:, hk * HEAD_DIM:(hk + ATT_KV) * HEAD_DIM]),
        bt(gb[:, (hk + ATT_KV) * HEAD_DIM:]), bt(gc[:, :IDX_H * IDX_D]),
        bt(gc[:, IDX_H * IDX_D:IDX_H * IDX_D + IDX_D]), bt(gs[:, 2 * hv:2 * hv + IDX_H]),
        pos, dn_conv_prev, dn_state, attend, p)
    x1 = _merge(rows_of(o_dn), rows_of(o_att), gd, x2, g1, wdn, watt, wo, tm)

    up = _norm_mm(x1, sc2, sh2, p['norm2_w'][None, :], wup, tm, 1408, "ffn_up")
    rs = B if time_major else 1
    if time_major:
        prev_rows = ffn_prev.swapaxes(0, 1).reshape((FFN_CONV_W - 1) * B, -1)
        new_ffn = up[rows - (FFN_CONV_W - 1) * B:].reshape(FFN_CONV_W - 1, B, -1).swapaxes(0, 1)
        tm_ffn = B
    else:
        prev_rows = ffn_prev.reshape(FFN_CONV_W - 1, -1)
        new_ffn = up[rows - (FFN_CONV_W - 1):][None]
        tm_ffn = 256
    y = _ffn_tail(up, prev_rows, p['ffn_conv_w'], p['ffn_conv_b'][None, :], wdown, x1, g2, tm_ffn, rs)
    return y, st + (new_ffn,)


def kernel(x_prompt, x_sample, c_prompt, c_sample, cache_k, cache_v, cache_kidx, state_dn, state_dn_conv, state_ffn_conv, page_table, w_ada, b_ada, norm1_w, w_in, dn_conv_w, dn_a_log, dn_dt_bias, dn_norm_w, q_norm_w, k_norm_w, idx_k_norm_w, w_dn_proj, w_att_proj, w_o, norm2_w, w_up, ffn_conv_w, ffn_conv_b, w_down):
    B, S, d = x_prompt.shape
    DB, T, _ = x_sample.shape
    depth = w_ada.shape[0]
    hk = d // HEAD_DIM
    hv = 2 * hk
    past = page_table.shape[1] * cache_k.shape[2]
    pos_p = jnp.arange(S, dtype=jnp.int32)
    pos_s = past + jnp.arange(T, dtype=jnp.int32)
    hp = x_prompt.reshape(B * S, d)
    hs = x_sample.swapaxes(0, 1).reshape(T * DB, d)
    outs_p, outs_s = [], []
    for l in range(depth):
        p = {'norm1_w': norm1_w[l], 'dn_conv_w': dn_conv_w[l], 'dn_a_log': dn_a_log[l],
             'dn_dt_bias': dn_dt_bias[l], 'dn_norm_w': dn_norm_w[l], 'q_norm_w': q_norm_w[l],
             'k_norm_w': k_norm_w[l], 'idx_k_norm_w': idx_k_norm_w[l], 'norm2_w': norm2_w[l],
             'ffn_conv_w': ffn_conv_w[l], 'ffn_conv_b': ffn_conv_b[l]}
        wts = _pack_w_in(w_in[l], d) + tuple(
            w.astype(BF16) for w in (w_dn_proj[l], w_att_proj[l], w_o[l], w_up[l], w_down[l]))
        n_c = DB + B
        n_pad = -(-n_c // 8) * 8
        c_all = jnp.concatenate([c_sample, c_prompt, jnp.zeros((n_pad - n_c, d), F32)], axis=0)
        mod = _ada(c_all, w_ada[l].astype(BF16), b_ada[l][None, :])
        mod_s, mod_p = mod[:DB], mod[DB:DB + B]
        hp, st_p = _layer(
            hp, mod_p, B, S, False, pos_p,
            jnp.zeros((B, DN_CONV_W - 1, 4 * hk * HEAD_DIM), F32),
            jnp.zeros((B, hv, HEAD_DIM, HEAD_DIM), F32),
            jnp.zeros((B, FFN_CONV_W - 1, w_up.shape[2]), F32),
            lambda q, k, v, qi, wi, ki: jnp.stack([
                _dsa_prompt_pallas(q[b], k[b], v[b], qi[b], wi[b], ki[b], min(TOPK_MAX, S // 4))
                for b in range(B)]),
            p, wts, 512)
        hs, st_s = _layer(
            hs, mod_s, DB, T, True, pos_s, state_dn_conv[l], state_dn[l], state_ffn_conv[l],
            lambda q, k, v, qi, wi, ki: _dsa_sample_pallas(q, k, v, qi, wi, ki, cache_k[l], cache_v[l],
                                                           cache_kidx[l], page_table,
                                                           min(TOPK_MAX, (past + T) // 4)),
            p, wts, DB)
        outs_p.append(st_p)
        outs_s.append(st_s)

    def stack(outs, i):
        return jnp.stack([o[i] for o in outs], axis=0)

    y_p = hp.reshape(B, S, d)
    y_s = hs.reshape(T, DB, d).swapaxes(0, 1)
    return (y_p, y_s,
            stack(outs_p, 0), stack(outs_p, 1), stack(outs_p, 2), stack(outs_p, 3), stack(outs_p, 4), stack(outs_p, 5),
            stack(outs_s, 0), stack(outs_s, 1), stack(outs_s, 2), stack(outs_s, 3), stack(outs_s, 4), stack(outs_s, 5))
```
